```python
import math
import jax, jax.numpy as jnp
from jax import lax
import numpy as np

D_MODEL = 2048
BATCH = 2
SEQ = 4096
DEPTH = 2
DEC_BATCH = 128
DEC_SEQ = 1
PAST_LEN = 2048
PAGE_SIZE = 128

SSM_INNER = D_MODEL // 2
SSM_HEADDIM = 64
SSM_HEADS = SSM_INNER // SSM_HEADDIM
SSM_GROUPS = 4
SSM_STATE = 128
SSM_CONV = 4
SSM_CHUNK = 128
SSM_XBC = SSM_INNER + 2 * SSM_GROUPS * SSM_STATE
FOX_HEADS = 8
FOX_HEADDIM = 128
FOX_WIDTH = FOX_HEADS * FOX_HEADDIM
Q_BLOCK = 128
SC_WIDTH = D_MODEL // 2
SC_CONV = 3
D_FF = 5632
FFN_CONV = 3
N_BRANCH = 3
DN_ALPHA = (2 * DEPTH) ** 0.25
DN_BETA = (8 * DEPTH) ** -0.25
LN_EPS = 1e-5
RMS_EPS = 1e-5

IN_SIZES = (SSM_INNER, SSM_XBC, SSM_HEADS, FOX_WIDTH, FOX_WIDTH, FOX_WIDTH, FOX_HEADS,
            SC_WIDTH, SC_WIDTH, SC_WIDTH, N_BRANCH * D_MODEL)
IN_TOTAL = sum(IN_SIZES)
IN_SPLITS = tuple(int(s) for s in np.cumsum(IN_SIZES)[:-1])

kernel_name = 'hybrid_ssd_fox_shortconv_decode_step'


def layer_norm(x, g, b):
    xf = x.astype(jnp.float32)
    mu = jnp.mean(xf, axis=-1, keepdims=True)
    var = jnp.mean(jnp.square(xf - mu), axis=-1, keepdims=True)
    return ((xf - mu) * lax.rsqrt(var + LN_EPS) * g + b).astype(x.dtype)


def causal_dwconv(u, w, prev):
    width = w.shape[0]
    L = u.shape[1]
    up = jnp.concatenate([prev.astype(u.dtype), u], axis=1)
    y = up[:, 0:L] * w[0]
    for j in range(1, width):
        y = y + up[:, j:j + L] * w[j]
    return y, up[:, L:]


def segsum(a):
    T = a.shape[-1]
    ae = jnp.broadcast_to(a[..., None], a.shape + (T,))
    ae = jnp.where(jnp.tril(jnp.ones((T, T), bool), -1), ae, 0.0)
    cs = jnp.cumsum(ae, axis=-2)
    return jnp.where(jnp.tril(jnp.ones((T, T), bool)), cs, -jnp.inf)


def ssd_chunked(xh, dt, A, Bh, Ch):
    Bsz, L, H, P = xh.shape
    nc = L // SSM_CHUNK
    chunk = lambda t: t.reshape((Bsz, nc, SSM_CHUNK) + t.shape[2:])
    xdt = chunk(xh * dt[..., None])
    Bc, Cc = chunk(Bh), chunk(Ch)
    a = chunk(dt * A).transpose(0, 3, 1, 2)
    a_cs = jnp.cumsum(a, axis=-1)
    Lmat = jnp.exp(segsum(a))
    y_diag = jnp.einsum('bclhn,bcshn,bhcls,bcshp->bclhp', Cc, Bc, Lmat, xdt)
    decay_states = jnp.exp(a_cs[..., -1:] - a_cs)
    states = jnp.einsum('bclhn,bhcl,bclhp->bchpn', Bc, decay_states, xdt)
    states = jnp.concatenate([jnp.zeros_like(states[:, :1]), states], axis=1)
    decay_chunk = jnp.exp(segsum(jnp.pad(a_cs[..., -1], ((0, 0), (0, 0), (1, 0)))))
    new_states = jnp.einsum('bhzc,bchpn->bzhpn', decay_chunk, states)
    states, final = new_states[:, :-1], new_states[:, -1]
    y_off = jnp.einsum('bclhn,bchpn,bhcl->bclhp', Cc, states, jnp.exp(a_cs))
    return (y_diag + y_off).reshape(Bsz, L, H, P), final


def ssd_recurrent(xh, dt, A, Bh, Ch, h0):
    def step(h, inp):
        x_t, dt_t, B_t, C_t = inp
        h = h * jnp.exp(dt_t * A)[..., None, None] + jnp.einsum('bhp,bhn->bhpn', x_t * dt_t[..., None], B_t)
        return h, jnp.einsum('bhpn,bhn->bhp', h, C_t)
    sf = lambda t: jnp.moveaxis(t, 1, 0)
    h, ys = lax.scan(step, h0, (sf(xh), sf(dt), sf(Bh), sf(Ch)))
    return jnp.moveaxis(ys, 0, 1), h


def gated_group_rmsnorm(y, z, w):
    h = (y * jax.nn.silu(z)).astype(jnp.float32)
    hg = h.reshape(h.shape[:-1] + (SSM_GROUPS, SSM_INNER // SSM_GROUPS))
    hg = hg * lax.rsqrt(jnp.mean(hg * hg, axis=-1, keepdims=True) + RMS_EPS)
    return (hg.reshape(h.shape) * w).astype(y.dtype)


def ssm_branch(z, xbc, dt_raw, p, conv_prev, h0):
    Bsz, L = z.shape[:2]
    f32 = jnp.float32
    if conv_prev is None:
        conv_prev = jnp.zeros((Bsz, SSM_CONV - 1, SSM_XBC), xbc.dtype)
    xbc_c, conv_new = causal_dwconv(xbc, p['ssm_conv_w'], conv_prev)
    xbc_c = jax.nn.silu(xbc_c + p['ssm_conv_b'])
    xs, Bs, Cs = jnp.split(xbc_c, [SSM_INNER, SSM_INNER + SSM_GROUPS * SSM_STATE], axis=-1)
    rep = SSM_HEADS // SSM_GROUPS
    xh = xs.reshape(Bsz, L, SSM_HEADS, SSM_HEADDIM).astype(f32)
    Bh = jnp.repeat(Bs.reshape(Bsz, L, SSM_GROUPS, SSM_STATE), rep, axis=2).astype(f32)
    Ch = jnp.repeat(Cs.reshape(Bsz, L, SSM_GROUPS, SSM_STATE), rep, axis=2).astype(f32)
    dt = jax.nn.softplus(dt_raw.astype(f32) + p['ssm_dt_bias'].astype(f32))
    A = -jnp.exp(p['ssm_a_log'].astype(f32))
    if h0 is None:
        y, h = ssd_chunked(xh, dt, A, Bh, Ch)
    else:
        y, h = ssd_recurrent(xh, dt, A, Bh, Ch, h0.astype(f32))
    y = y + xh * p['ssm_d'].astype(f32)[:, None]
    y = y.reshape(Bsz, L, SSM_INNER).astype(z.dtype)
    y = gated_group_rmsnorm(y, z, p['ssm_norm_w'])
    return y @ p['w_ssm_out'], conv_new, h


def fox_prompt_attention(q, k, v, logf):
    Bsz, L, H, Dh = q.shape
    scale = Dh ** -0.5
    c = jnp.cumsum(logf, axis=1).transpose(0, 2, 1)
    nb = L // Q_BLOCK
    qb = q.reshape(Bsz, nb, Q_BLOCK, H, Dh).transpose(1, 0, 2, 3, 4)
    cb = c.reshape(Bsz, H, nb, Q_BLOCK).transpose(2, 0, 1, 3)
    kpos = jnp.arange(L)

    def block(args):
        i, q_i, c_i = args
        s = jnp.einsum('bqhd,bkhd->bhqk', q_i, k).astype(jnp.float32) * scale
        s = s + (c_i[..., :, None] - c[..., None, :])
        qpos = i * Q_BLOCK + jnp.arange(Q_BLOCK)
        s = jnp.where(kpos[None, :] <= qpos[:, None], s, -jnp.inf)
        pr = jax.nn.softmax(s, axis=-1)
        return jnp.einsum('bhqk,bkhd->bqhd', pr.astype(v.dtype), v)

    o = lax.map(block, (jnp.arange(nb), qb, cb))
    return o.transpose(1, 0, 2, 3, 4).reshape(Bsz, L, H, Dh)


def fox_sample_attention(q, k, v, logf, ck, cv, clogf, page_table):
    DB, T, H, Dh = q.shape
    scale = Dh ** -0.5
    past = page_table.shape[1] * ck.shape[1]
    k_past = ck[page_table].reshape(DB, past, H, Dh)
    v_past = cv[page_table].reshape(DB, past, H, Dh)
    lf_past = clogf[page_table].reshape(DB, past, H).astype(jnp.float32)
    c = jnp.cumsum(jnp.concatenate([lf_past, logf], axis=1), axis=1).transpose(0, 2, 1)
    bias = c[..., past:, None] - c[..., None, :]
    s = jnp.concatenate([jnp.einsum('bqhd,bkhd->bhqk', q, k_past),
                         jnp.einsum('bqhd,bkhd->bhqk', q, k)], axis=-1).astype(jnp.float32) * scale + bias
    mask = jnp.concatenate([jnp.ones((T, past), bool), jnp.tril(jnp.ones((T, T), bool))], axis=-1)
    pr = jax.nn.softmax(jnp.where(mask, s, -jnp.inf), axis=-1).astype(v.dtype)
    return (jnp.einsum('bhqk,bkhd->bqhd', pr[..., :past], v_past)
            + jnp.einsum('bhqk,bkhd->bqhd', pr[..., past:], v))


def fox_branch(q, k, v, f_raw, p, past):
    Bsz, L = q.shape[:2]
    qh = q.reshape(Bsz, L, FOX_HEADS, FOX_HEADDIM)
    kh = k.reshape(Bsz, L, FOX_HEADS, FOX_HEADDIM)
    vh = v.reshape(Bsz, L, FOX_HEADS, FOX_HEADDIM)
    logf = jax.nn.log_sigmoid(f_raw.astype(jnp.float32) + p['fox_b_f'].astype(jnp.float32))
    if past is None:
        o = fox_prompt_attention(qh, kh, vh, logf)
    else:
        o = fox_sample_attention(qh, kh, vh, logf, past['k'], past['v'], past['logf'], past['page_table'])
    return o.reshape(Bsz, L, FOX_WIDTH) @ p['w_fox_out'], kh, vh, logf


def shortconv_branch(bg, cg, xin, p, prev):
    u = cg * xin
    if prev is None:
        prev = jnp.zeros((u.shape[0], SC_CONV - 1, SC_WIDTH), u.dtype)
    y, new = causal_dwconv(u, p['sc_conv_w'], prev)
    return (bg * y) @ p['w_sc_out'], new


def mixer(x, p, past):
    proj = x @ p['w_in']
    z, xbc, dt_raw, q, k, v, f_raw, sc_b, sc_c, sc_x, gate = jnp.split(proj, IN_SPLITS, axis=-1)
    prompt = past is None
    y_a, ssm_conv, h = ssm_branch(z, xbc, dt_raw, p, None if prompt else past['ssm_conv'],
                                  None if prompt else past['ssm'])
    y_b, kh, vh, logf = fox_branch(q, k, v, f_raw, p, past)
    y_c, sc_conv = shortconv_branch(sc_b, sc_c, sc_x, p, None if prompt else past['sc_conv'])
    g_a, g_b, g_c = jnp.split(jax.nn.sigmoid(gate + p['b_gate']), N_BRANCH, axis=-1)
    y = (g_a * y_a + g_b * y_b + g_c * y_c) @ p['w_o']
    return y, (kh, vh, logf, h, ssm_conv, sc_conv)


def conv_ffn(x, p, prev):
    a, b = jnp.split(x @ p['w_up'], 2, axis=-1)
    if prev is None:
        prev = jnp.zeros((a.shape[0], FFN_CONV - 1, D_FF), a.dtype)
    a_c, new = causal_dwconv(a, p['ffn_conv_w'], prev)
    h = jax.nn.gelu(a_c + p['ffn_conv_b']) * b
    return h @ p['w_down'], new


def layer(x, p, past):
    m, (kh, vh, logf, h, ssm_conv, sc_conv) = mixer(x, p, past)
    x = layer_norm(DN_ALPHA * x + m, p['ln1_g'], p['ln1_b'])
    f, ffn_conv = conv_ffn(x, p, None if past is None else past['ffn_conv'])
    x = layer_norm(DN_ALPHA * x + f, p['ln2_g'], p['ln2_b'])
    return x, (kh, vh, logf, h, ssm_conv, sc_conv, ffn_conv)


def setup_inputs(seed: int = 0) -> dict:
    key = jax.random.key(seed)
    ks = iter(jax.random.split(key, 48))
    f32 = jnp.float32
    n_pages = PAST_LEN // PAGE_SIZE
    n_phys = (DEC_BATCH * n_pages * 5) // 4

    def normal(shape, scale):
        return scale * jax.random.normal(next(ks), shape, f32)

    x_prompt = normal((BATCH, SEQ, D_MODEL), 1.0)
    x_sample = normal((DEC_BATCH, DEC_SEQ, D_MODEL), 1.0)
    cache_k = normal((DEPTH, n_phys, PAGE_SIZE, FOX_HEADS, FOX_HEADDIM), 1.0)
    cache_v = normal((DEPTH, n_phys, PAGE_SIZE, FOX_HEADS, FOX_HEADDIM), 1.0)
    cache_logf = jax.nn.log_sigmoid(3.0 + normal((DEPTH, n_phys, PAGE_SIZE, FOX_HEADS), 1.0))
    state_ssm = normal((DEPTH, DEC_BATCH, SSM_HEADS, SSM_HEADDIM, SSM_STATE), 0.1)
    state_ssm_conv = normal((DEPTH, DEC_BATCH, SSM_CONV - 1, SSM_XBC), 1.0)
    state_sc_conv = normal((DEPTH, DEC_BATCH, SC_CONV - 1, SC_WIDTH), 1.0)
    state_ffn_conv = normal((DEPTH, DEC_BATCH, FFN_CONV - 1, D_FF), 1.0)
    page_table = jax.random.permutation(next(ks), n_phys)[:DEC_BATCH * n_pages].reshape(
        DEC_BATCH, n_pages).astype(jnp.int32)

    w_in = normal((DEPTH, D_MODEL, IN_TOTAL), D_MODEL ** -0.5)
    b_gate = normal((DEPTH, N_BRANCH * D_MODEL), 0.02)
    ssm_conv_w = normal((DEPTH, SSM_CONV, SSM_XBC), SSM_CONV ** -0.5)
    ssm_conv_b = normal((DEPTH, SSM_XBC), 0.02)
    dt0 = jnp.exp(jax.random.uniform(next(ks), (DEPTH, SSM_HEADS), f32, math.log(1e-3), math.log(1e-1)))
    ssm_dt_bias = dt0 + jnp.log(-jnp.expm1(-dt0))
    ssm_a_log = jnp.log(jax.random.uniform(next(ks), (DEPTH, SSM_HEADS), f32, 1.0, 16.0))
    ssm_d = 1.0 + normal((DEPTH, SSM_HEADS), 0.02)
    ssm_norm_w = 1.0 + normal((DEPTH, SSM_INNER), 0.02)
    w_ssm_out = normal((DEPTH, SSM_INNER, D_MODEL), DN_BETA * SSM_INNER ** -0.5)
    fox_b_f = 3.0 + normal((DEPTH, FOX_HEADS), 0.5)
    w_fox_out = normal((DEPTH, FOX_WIDTH, D_MODEL), DN_BETA * FOX_WIDTH ** -0.5)
    sc_conv_w = normal((DEPTH, SC_CONV, SC_WIDTH), SC_CONV ** -0.5)
    w_sc_out = normal((DEPTH, SC_WIDTH, D_MODEL), DN_BETA * SC_WIDTH ** -0.5)
    w_o = normal((DEPTH, D_MODEL, D_MODEL), DN_BETA * D_MODEL ** -0.5)
    ln1_g = 1.0 + normal((DEPTH, D_MODEL), 0.02)
    ln1_b = normal((DEPTH, D_MODEL), 0.02)
    w_up = normal((DEPTH, D_MODEL, 2 * D_FF), D_MODEL ** -0.5)
    ffn_conv_w = normal((DEPTH, FFN_CONV, D_FF), FFN_CONV ** -0.5)
    ffn_conv_b = normal((DEPTH, D_FF), 0.02)
    w_down = normal((DEPTH, D_FF, D_MODEL), DN_BETA * D_FF ** -0.5)
    ln2_g = 1.0 + normal((DEPTH, D_MODEL), 0.02)
    ln2_b = normal((DEPTH, D_MODEL), 0.02)
    return {'x_prompt': x_prompt, 'x_sample': x_sample,
            'cache_k': cache_k, 'cache_v': cache_v, 'cache_logf': cache_logf,
            'state_ssm': state_ssm, 'state_ssm_conv': state_ssm_conv,
            'state_sc_conv': state_sc_conv, 'state_ffn_conv': state_ffn_conv,
            'page_table': page_table,
            'w_in': w_in, 'b_gate': b_gate, 'ssm_conv_w': ssm_conv_w, 'ssm_conv_b': ssm_conv_b,
            'ssm_dt_bias': ssm_dt_bias, 'ssm_a_log': ssm_a_log, 'ssm_d': ssm_d, 'ssm_norm_w': ssm_norm_w,
            'w_ssm_out': w_ssm_out, 'fox_b_f': fox_b_f, 'w_fox_out': w_fox_out,
            'sc_conv_w': sc_conv_w, 'w_sc_out': w_sc_out, 'w_o': w_o,
            'ln1_g': ln1_g, 'ln1_b': ln1_b, 'w_up': w_up, 'ffn_conv_w': ffn_conv_w,
            'ffn_conv_b': ffn_conv_b, 'w_down': w_down, 'ln2_g': ln2_g, 'ln2_b': ln2_b}


def reference(x_prompt, x_sample, cache_k, cache_v, cache_logf, state_ssm, state_ssm_conv,
              state_sc_conv, state_ffn_conv, page_table, w_in, b_gate, ssm_conv_w, ssm_conv_b,
              ssm_dt_bias, ssm_a_log, ssm_d, ssm_norm_w, w_ssm_out, fox_b_f, w_fox_out,
              sc_conv_w, w_sc_out, w_o, ln1_g, ln1_b, w_up, ffn_conv_w, ffn_conv_b, w_down,
              ln2_g, ln2_b):
    yp, ys = x_prompt, x_sample
    st_p, st_s = [], []
    for l in range(DEPTH):
        p = {'w_in': w_in[l], 'b_gate': b_gate[l], 'ssm_conv_w': ssm_conv_w[l], 'ssm_conv_b': ssm_conv_b[l],
             'ssm_dt_bias': ssm_dt_bias[l], 'ssm_a_log': ssm_a_log[l], 'ssm_d': ssm_d[l],
             'ssm_norm_w': ssm_norm_w[l], 'w_ssm_out': w_ssm_out[l], 'fox_b_f': fox_b_f[l],
             'w_fox_out': w_fox_out[l], 'sc_conv_w': sc_conv_w[l], 'w_sc_out': w_sc_out[l], 'w_o': w_o[l],
             'ln1_g': ln1_g[l], 'ln1_b': ln1_b[l], 'w_up': w_up[l], 'ffn_conv_w': ffn_conv_w[l],
             'ffn_conv_b': ffn_conv_b[l], 'w_down': w_down[l], 'ln2_g': ln2_g[l], 'ln2_b': ln2_b[l]}
        yp, sp = layer(yp, p, None)
        past = {'k': cache_k[l], 'v': cache_v[l], 'logf': cache_logf[l], 'page_table': page_table,
                'ssm': state_ssm[l], 'ssm_conv': state_ssm_conv[l], 'sc_conv': state_sc_conv[l],
                'ffn_conv': state_ffn_conv[l]}
        ys, ss = layer(ys, p, past)
        st_p.append(sp)
        st_s.append(ss)
    stk = lambda lst, i: jnp.stack([s[i] for s in lst])
    return (yp, ys,
            stk(st_p, 0), stk(st_p, 1), stk(st_p, 2), stk(st_p, 3), stk(st_p, 4), stk(st_p, 5), stk(st_p, 6),
            stk(st_s, 0), stk(st_s, 1), stk(st_s, 2), stk(st_s, 3), stk(st_s, 4), stk(st_s, 5), stk(st_s, 6))
```

```python
import functools
import math

import jax
import jax.numpy as jnp
from jax import lax
from jax.experimental import pallas as pl
from jax.experimental.pallas import tpu as pltpu

F32 = jnp.float32
BF16 = jnp.bfloat16
HIGHEST = lax.Precision.HIGHEST

LN_EPS = 1e-5
RMS_EPS = 1e-5
SSM_HEADDIM = 64
SSM_GROUPS = 4
SSM_STATE = 128
SSM_CHUNK = 128
FOX_HEADDIM = 128
N_BRANCH = 3
NEG_BIG = -1e30

LANES = 128
SUBLANES = 8
VMEM_CAP = 60 * 1024 * 1024
VMEM_SLACK = 12 * 1024 * 1024


def _nbytes(shape, dtype):
    return math.prod(shape) * jnp.dtype(dtype).itemsize


def _params(block_bytes, scratch_bytes=0, semantics=None):
    need = 2 * sum(block_bytes) + scratch_bytes + VMEM_SLACK
    return pltpu.CompilerParams(dimension_semantics=semantics,
                                vmem_limit_bytes=min(max(need, 32 * 1024 * 1024), VMEM_CAP))


def _pick(n, pref):
    if n <= pref:
        return n
    t = pref
    while n % t:
        t -= LANES if t > LANES else SUBLANES
    return t


def _softplus(v):
    return jnp.maximum(v, 0.0) + jnp.log1p(jnp.exp(-jnp.abs(v)))


def _silu(v):
    return v * (1.0 / (1.0 + jnp.exp(-v)))


def _gelu_tanh(v):
    c = math.sqrt(2.0 / math.pi)
    return 0.5 * v * (1.0 + jnp.tanh(c * (v + 0.044715 * (v * v * v))))


def _mm_kernel(a_ref, b_ref, o_ref, a_bf):
    @pl.when(pl.program_id(1) == 0)
    def _():
        a_bf[...] = a_ref[...].astype(BF16)

    o_ref[...] = jnp.dot(a_bf[...], b_ref[...], preferred_element_type=F32).astype(o_ref.dtype)


def mm_plain(a, b, *, tm=1024, tn=512, out_dtype=F32):
    M, K = a.shape
    N = b.shape[1]
    tm, tn = _pick(M, tm), _pick(N, tn)
    blocks = [_nbytes((tm, K), a.dtype), _nbytes((K, tn), BF16), _nbytes((tm, tn), out_dtype)]
    return pl.pallas_call(
        _mm_kernel,
        grid=(M // tm, N // tn),
        in_specs=[pl.BlockSpec((tm, K), lambda i, j: (i, 0)),
                  pl.BlockSpec((K, tn), lambda i, j: (0, j))],
        out_specs=pl.BlockSpec((tm, tn), lambda i, j: (i, j)),
        out_shape=jax.ShapeDtypeStruct((M, N), out_dtype),
        scratch_shapes=[pltpu.VMEM((tm, K), BF16)],
        compiler_params=_params(blocks, _nbytes((tm, K), BF16), ("parallel", "arbitrary")),
        name="mm_plain",
    )(a, b)


def _mm_gated3_kernel(a1, a2, a3, w1, w2, w3, g1, g2, g3, b1, b2, b3, o_ref):
    acc = None
    for a, w, g, b in ((a1, w1, g1, b1), (a2, w2, g2, b2), (a3, w3, g3, b3)):
        y = jnp.dot(a[...].astype(BF16), w[...], preferred_element_type=F32)
        gate = 1.0 / (1.0 + jnp.exp(-(g[...] + b[...])))
        acc = gate * y if acc is None else acc + gate * y
    o_ref[...] = acc.astype(o_ref.dtype)


def mm_gated3(branches, weights, proj, gate_col0, b_gate, *, tm=512, tn=512):
    M, K = branches[0].shape
    N = weights[0].shape[1]
    tm, tn = _pick(M, tm), _pick(N, tn)
    nj = N // tn
    g0 = gate_col0 // tn
    a_specs = [pl.BlockSpec((tm, K), lambda i, j: (i, 0)) for _ in range(3)]
    w_specs = [pl.BlockSpec((K, tn), lambda i, j: (0, j)) for _ in range(3)]
    g_specs = [pl.BlockSpec((tm, tn), functools.partial(lambda i, j, br: (i, g0 + br * nj + j), br=br))
               for br in range(3)]
    b_specs = [pl.BlockSpec((1, tn), functools.partial(lambda i, j, br: (0, br * nj + j), br=br))
               for br in range(3)]
    blocks = ([_nbytes((tm, K), branches[0].dtype)] * 3 + [_nbytes((K, tn), BF16)] * 3
              + [_nbytes((tm, tn), F32)] * 3 + [_nbytes((tm, tn), BF16)])
    return pl.pallas_call(
        _mm_gated3_kernel,
        grid=(M // tm, nj),
        in_specs=a_specs + w_specs + g_specs + b_specs,
        out_specs=pl.BlockSpec((tm, tn), lambda i, j: (i, j)),
        out_shape=jax.ShapeDtypeStruct((M, N), BF16),
        compiler_params=_params(blocks, 0, ("parallel", "arbitrary")),
        name="mm_gated3",
    )(*branches, *weights, proj, proj, proj, b_gate, b_gate, b_gate)


def _mm_ln_kernel(a_ref, b_ref, r_ref, g_ref, be_ref, o_ref, acc, *, alpha, nk):
    k = pl.program_id(1)
    part = jnp.dot(a_ref[...].astype(BF16), b_ref[...], preferred_element_type=F32)

    @pl.when(k == 0)
    def _():
        acc[...] = part

    @pl.when(k > 0)
    def _():
        acc[...] += part

    @pl.when(k == nk - 1)
    def _():
        v = alpha * r_ref[...] + acc[...]
        mu = jnp.mean(v, axis=-1, keepdims=True)
        d = v - mu
        var = jnp.mean(d * d, axis=-1, keepdims=True)
        o_ref[...] = d * lax.rsqrt(var + LN_EPS) * g_ref[...] + be_ref[...]


def mm_ln(a, b, resid, gamma, beta, alpha, *, tm=512, tk=1024):
    M, K = a.shape
    N = b.shape[1]
    tm, tk = _pick(M, tm), _pick(K, tk)
    nk = K // tk
    blocks = [_nbytes((tm, tk), a.dtype), _nbytes((tk, N), BF16), _nbytes((tm, N), F32) * 2]
    return pl.pallas_call(
        functools.partial(_mm_ln_kernel, alpha=alpha, nk=nk),
        grid=(M // tm, nk),
        in_specs=[pl.BlockSpec((tm, tk), lambda i, k: (i, k)),
                  pl.BlockSpec((tk, N), lambda i, k: (k, 0)),
                  pl.BlockSpec((tm, N), lambda i, k: (i, 0)),
                  pl.BlockSpec((1, N), lambda i, k: (0, 0)),
                  pl.BlockSpec((1, N), lambda i, k: (0, 0))],
        out_specs=pl.BlockSpec((tm, N), lambda i, k: (i, 0)),
        out_shape=jax.ShapeDtypeStruct((M, N), F32),
        scratch_shapes=[pltpu.VMEM((tm, N), F32)],
        compiler_params=_params(blocks, _nbytes((tm, N), F32), ("parallel", "arbitrary")),
        name="mm_ln",
    )(a, b, resid, gamma, beta)


def _conv_kernel(*refs, width, has_m, has_bias, has_post, act):
    it = iter(refs)
    a, ah = next(it), next(it)
    m, mh = (next(it), next(it)) if has_m else (None, None)
    w = next(it)
    bias = next(it) if has_bias else None
    post = next(it) if has_post else None
    y_ref, tail_ref = next(it), next(it)
    t = pl.program_id(2)
    tt = a.shape[0]

    u = a[...]
    uh = ah[...]
    if has_m:
        u = u * m[...]
        uh = uh * mh[...]
    uh = jnp.where(t == 0, 0.0, uh)
    ext = jnp.concatenate([uh, u], axis=0)
    acc = None
    for j in range(width):
        delay = width - 1 - j
        tap = ext if delay == 0 else pltpu.roll(ext, delay, 0)
        term = tap[SUBLANES:SUBLANES + tt] * w[j:j + 1, :]
        acc = term if acc is None else acc + term
    if has_bias:
        acc = acc + bias[...]
    if act == "silu":
        acc = _silu(acc)
    elif act == "gelu":
        acc = _gelu_tanh(acc)
    if has_post:
        acc = acc * post[...]
    y_ref[...] = acc.astype(y_ref.dtype)

    @pl.when(t == pl.num_programs(2) - 1)
    def _():
        tail_ref[...] = u[tt - SUBLANES:, :]


def conv_act(G, T, C, a, a_col0, w, *, m=None, m_col0=0, bias=None, post=None, post_col0=0,
             act=None, out_dtype=F32, tt=512, tc=512):
    tt, tc = _pick(T, tt), _pick(C, tc)
    nt, ncb = T // tt, C // tc
    hb = tt // SUBLANES

    def main_spec(col0):
        return pl.BlockSpec((tt, tc), lambda g, c, t: (g * nt + t, col0 // tc + c))

    def halo_spec(col0):
        return pl.BlockSpec((SUBLANES, tc),
                            lambda g, c, t: (jnp.maximum((g * nt + t) * hb - 1, 0), col0 // tc + c))

    row_spec = pl.BlockSpec((1, tc), lambda g, c, t: (0, c))
    args, specs = [a, a], [main_spec(a_col0), halo_spec(a_col0)]
    if m is not None:
        args += [m, m]
        specs += [main_spec(m_col0), halo_spec(m_col0)]
    args.append(w)
    specs.append(pl.BlockSpec((w.shape[0], tc), lambda g, c, t: (0, c)))
    if bias is not None:
        args.append(bias)
        specs.append(row_spec)
    if post is not None:
        args.append(post)
        specs.append(main_spec(post_col0))
    blocks = [_nbytes((tt, tc), F32)] * 4
    y, tail = pl.pallas_call(
        functools.partial(_conv_kernel, width=w.shape[0], has_m=m is not None,
                          has_bias=bias is not None, has_post=post is not None, act=act),
        grid=(G, ncb, nt),
        in_specs=specs,
        out_specs=[pl.BlockSpec((tt, tc), lambda g, c, t: (g * nt + t, c)),
                   pl.BlockSpec((None, SUBLANES, tc), lambda g, c, t: (g, 0, c))],
        out_shape=[jax.ShapeDtypeStruct((G * T, C), out_dtype),
                   jax.ShapeDtypeStruct((G, SUBLANES, C), F32)],
        compiler_params=_params(blocks, 0, ("parallel", "parallel", "arbitrary")),
        name="conv_act",
    )(*args)
    return y, tail


def _small_kernel(x_ref, b_ref, act_ref, cum_ref, carry, *, n_dt):
    t = pl.program_id(1)
    tt = x_ref.shape[0]
    v = x_ref[...] + b_ref[...]
    lane = lax.broadcasted_iota(jnp.int32, v.shape, 1)
    act = jnp.where(lane < n_dt, _softplus(v), -_softplus(-v))
    act_ref[...] = act

    @pl.when(t == 0)
    def _():
        carry[...] = jnp.zeros_like(carry)

    r = lax.broadcasted_iota(jnp.int32, (tt, tt), 0)
    c = lax.broadcasted_iota(jnp.int32, (tt, tt), 1)
    tri = (c <= r).astype(F32)
    cum = jnp.dot(tri, act, preferred_element_type=F32, precision=HIGHEST) + carry[...]
    cum_ref[...] = cum
    carry[...] = cum[tt - 1:tt, :]


def small_ops(G, T, x, bias_row, n_dt, *, tt=256):
    tt = _pick(T, tt)
    nt = T // tt
    spec = pl.BlockSpec((tt, LANES), lambda g, t: (g * nt + t, 0))
    return pl.pallas_call(
        functools.partial(_small_kernel, n_dt=n_dt),
        grid=(G, nt),
        in_specs=[spec, pl.BlockSpec((1, LANES), lambda g, t: (0, 0))],
        out_specs=[spec, spec],
        out_shape=[jax.ShapeDtypeStruct((G * T, LANES), F32)] * 2,
        scratch_shapes=[pltpu.VMEM((1, LANES), F32)],
        compiler_params=_params([_nbytes((tt, LANES), F32)] * 3, 0, ("parallel", "arbitrary")),
        name="small_ops",
    )(x, bias_row)


def _gated_group_norm(y, z, w):
    hv = y * _silu(z)
    ms = jnp.mean(hv * hv, axis=-1, keepdims=True)
    return hv * lax.rsqrt(ms + RMS_EPS) * w


def _ssd_kernel(x_ref, b_ref, c_ref, dt_ref, z_ref, alog_ref, d_ref, nw_ref, y_ref, st_ref, st,
                *, heads):
    cidx = pl.program_id(1)
    L = x_ref.shape[0]
    P, N = SSM_HEADDIM, SSM_STATE
    hpg = heads // SSM_GROUPS

    @pl.when(cidx == 0)
    def _():
        st[...] = jnp.zeros_like(st)

    dt = dt_ref[...]
    a = dt * (-jnp.exp(alog_ref[...]))
    row = lax.broadcasted_iota(jnp.int32, (L, L), 0)
    col = lax.broadcasted_iota(jnp.int32, (L, L), 1)
    causal = col <= row
    cs = jnp.dot(causal.astype(F32), a, preferred_element_type=F32, precision=HIGHEST)
    cs_t = jnp.dot(a.T, (row <= col).astype(F32), preferred_element_type=F32, precision=HIGHEST)
    cs_last = cs[L - 1:L, :]

    for g in range(SSM_GROUPS):
        bg = b_ref[:, g * N:(g + 1) * N]
        cg = c_ref[:, g * N:(g + 1) * N].astype(BF16)
        bg_t = bg.T.astype(BF16)
        gmat = jnp.dot(cg, bg_t, preferred_element_type=F32)
        ys = []
        for hh in range(hpg):
            h = g * hpg + hh
            cs_col = cs[:, h:h + 1]
            cs_row = cs_t[h:h + 1, :]
            lmat = jnp.exp(jnp.where(causal, cs_col - cs_row, NEG_BIG))
            xh = x_ref[:, h * P:(h + 1) * P]
            xdt = xh * dt[:, h:h + 1]
            y_diag = jnp.dot((gmat * lmat).astype(BF16), xdt.astype(BF16), preferred_element_type=F32)
            s_t = st[h]
            y_off = jnp.dot(cg, s_t.astype(BF16), preferred_element_type=F32) * jnp.exp(cs_col)
            last = cs_last[:, h:h + 1]
            upd = jnp.dot(bg_t, (xdt * jnp.exp(last - cs_col)).astype(BF16), preferred_element_type=F32)
            st[h] = s_t * jnp.exp(last) + upd
            ys.append(y_diag + y_off + xh * d_ref[:, h * P:(h + 1) * P])
        gw = hpg * P
        yg = jnp.concatenate(ys, axis=1)
        out = _gated_group_norm(yg, z_ref[:, g * gw:(g + 1) * gw], nw_ref[:, g * gw:(g + 1) * gw])
        y_ref[:, g * gw:(g + 1) * gw] = out.astype(y_ref.dtype)

    @pl.when(cidx == pl.num_programs(1) - 1)
    def _():
        st_ref[...] = st[...]


def ssd_prompt(G, T, xbc_c, dt_act, proj, z_col0, a_log_row, d_row, norm_w_row, heads):
    L = SSM_CHUNK
    nc = T // L
    inner = heads * SSM_HEADDIM
    bc = SSM_GROUPS * SSM_STATE

    def rows(g, c):
        return g * nc + c

    blocks = [_nbytes((L, inner), F32) * 3, _nbytes((L, bc), F32) * 2]
    return pl.pallas_call(
        functools.partial(_ssd_kernel, heads=heads),
        grid=(G, nc),
        in_specs=[pl.BlockSpec((L, inner), lambda g, c: (rows(g, c), 0)),
                  pl.BlockSpec((L, bc), lambda g, c: (rows(g, c), inner // bc)),
                  pl.BlockSpec((L, bc), lambda g, c: (rows(g, c), inner // bc + 1)),
                  pl.BlockSpec((L, LANES), lambda g, c: (rows(g, c), 0)),
                  pl.BlockSpec((L, inner), lambda g, c: (rows(g, c), z_col0 // inner)),
                  pl.BlockSpec((1, LANES), lambda g, c: (0, 0)),
                  pl.BlockSpec((1, inner), lambda g, c: (0, 0)),
                  pl.BlockSpec((1, inner), lambda g, c: (0, 0))],
        out_specs=[pl.BlockSpec((L, inner), lambda g, c: (rows(g, c), 0)),
                   pl.BlockSpec((None, heads, SSM_STATE, SSM_HEADDIM), lambda g, c: (g, 0, 0, 0))],
        out_shape=[jax.ShapeDtypeStruct((G * T, inner), BF16),
                   jax.ShapeDtypeStruct((G, heads, SSM_STATE, SSM_HEADDIM), F32)],
        scratch_shapes=[pltpu.VMEM((heads, SSM_STATE, SSM_HEADDIM), F32)],
        compiler_params=_params(blocks, _nbytes((heads, SSM_STATE, LANES), F32), ("parallel", "arbitrary")),
        name="ssd_prompt",
    )(xbc_c, xbc_c, xbc_c, dt_act, proj, a_log_row, d_row, norm_w_row)


def _fox_kernel(q_ref, k_ref, v_ref, cq_ref, ck_ref, o_ref, m_s, l_s, acc_s, cq_s, *, lf_lane0, scale):
    h = pl.program_id(1)
    i = pl.program_id(2)
    j = pl.program_id(3)
    tq, tk = q_ref.shape[0], k_ref.shape[0]

    @pl.when(j == 0)
    def _():
        m_s[...] = jnp.full_like(m_s, NEG_BIG)
        l_s[...] = jnp.zeros_like(l_s)
        acc_s[...] = jnp.zeros_like(acc_s)
        lane = lax.broadcasted_iota(jnp.int32, cq_ref.shape, 1)
        cq_s[...] = jnp.sum(jnp.where(lane == lf_lane0 + h, cq_ref[...], 0.0), axis=1, keepdims=True)

    @pl.when(j * tk <= i * tq + tq - 1)
    def _():
        s = lax.dot_general(q_ref[...].astype(BF16), k_ref[...].astype(BF16),
                            (((1,), (1,)), ((), ())), preferred_element_type=F32) * scale
        s = s + (cq_s[...] - ck_ref[pl.ds(h, 1), :])
        qpos = i * tq + lax.broadcasted_iota(jnp.int32, (tq, tk), 0)
        kpos = j * tk + lax.broadcasted_iota(jnp.int32, (tq, tk), 1)
        s = jnp.where(kpos <= qpos, s, NEG_BIG)
        m_new = jnp.maximum(m_s[...], jnp.max(s, axis=1, keepdims=True))
        p = jnp.exp(s - m_new)
        corr = jnp.exp(m_s[...] - m_new)
        l_s[...] = corr * l_s[...] + jnp.sum(p, axis=1, keepdims=True)
        acc_s[...] = corr * acc_s[...] + jnp.dot(p.astype(BF16), v_ref[...].astype(BF16),
                                                 preferred_element_type=F32)
        m_s[...] = m_new

    @pl.when(j == pl.num_programs(3) - 1)
    def _():
        o_ref[...] = (acc_s[...] / l_s[...]).astype(o_ref.dtype)


def fox_prompt(G, T, proj, q_col0, k_col0, v_col0, cum, cum_t, lf_lane0, heads, *, tq=512):
    D = FOX_HEADDIM
    tq = _pick(T, tq)
    nq = T // tq

    def kv_block(g, i, j):
        return g * nq + jnp.minimum(j, i)

    blocks = [_nbytes((tq, D), F32) * 5]
    return pl.pallas_call(
        functools.partial(_fox_kernel, lf_lane0=lf_lane0, scale=D ** -0.5),
        grid=(G, heads, nq, nq),
        in_specs=[pl.BlockSpec((tq, D), lambda g, h, i, j: (g * nq + i, q_col0 // D + h)),
                  pl.BlockSpec((tq, D), lambda g, h, i, j: (kv_block(g, i, j), k_col0 // D + h)),
                  pl.BlockSpec((tq, D), lambda g, h, i, j: (kv_block(g, i, j), v_col0 // D + h)),
                  pl.BlockSpec((tq, LANES), lambda g, h, i, j: (g * nq + i, 0)),
                  pl.BlockSpec((SUBLANES, tq),
                               lambda g, h, i, j: (lf_lane0 // SUBLANES, kv_block(g, i, j)))],
        out_specs=pl.BlockSpec((tq, D), lambda g, h, i, j: (g * nq + i, h)),
        out_shape=jax.ShapeDtypeStruct((G * T, heads * D), BF16),
        scratch_shapes=[pltpu.VMEM((tq, 1), F32), pltpu.VMEM((tq, 1), F32),
                        pltpu.VMEM((tq, D), F32), pltpu.VMEM((tq, 1), F32)],
        compiler_params=_params(blocks, 4 * _nbytes((tq, LANES), F32),
                                ("parallel", "parallel", "parallel", "arbitrary")),
        name="fox_prompt",
    )(proj, proj, proj, cum, cum_t)


def _sconv_kernel(*refs, width, has_m, has_bias, has_post, act):
    it = iter(refs)
    prev, a = next(it), next(it)
    m = next(it) if has_m else None
    w = next(it)
    bias = next(it) if has_bias else None
    post = next(it) if has_post else None
    y_ref, u_ref = next(it), next(it)
    u = a[...]
    if has_m:
        u = u * m[...]
    u_ref[...] = u
    acc = prev[0] * w[0:1, :]
    for j in range(1, width - 1):
        acc = acc + prev[j] * w[j:j + 1, :]
    acc = acc + u * w[width - 1:width, :]
    if has_bias:
        acc = acc + bias[...]
    if act == "silu":
        acc = _silu(acc)
    elif act == "gelu":
        acc = _gelu_tanh(acc)
    if has_post:
        acc = acc * post[...]
    y_ref[...] = acc.astype(y_ref.dtype)


def sconv_act(prev_t, C, a, a_col0, w, *, m=None, m_col0=0, bias=None, post=None, post_col0=0,
              act=None, out_dtype=F32, tc=512):
    B = prev_t.shape[1]
    tc = _pick(C, tc)

    def spec(col0):
        return pl.BlockSpec((B, tc), lambda c: (0, col0 // tc + c))

    row_spec = pl.BlockSpec((1, tc), lambda c: (0, c))
    args = [prev_t, a]
    specs = [pl.BlockSpec((prev_t.shape[0], B, tc), lambda c: (0, 0, c)), spec(a_col0)]
    if m is not None:
        args.append(m)
        specs.append(spec(m_col0))
    args.append(w)
    specs.append(pl.BlockSpec((w.shape[0], tc), lambda c: (0, c)))
    if bias is not None:
        args.append(bias)
        specs.append(row_spec)
    if post is not None:
        args.append(post)
        specs.append(spec(post_col0))
    return pl.pallas_call(
        functools.partial(_sconv_kernel, width=w.shape[0], has_m=m is not None,
                          has_bias=bias is not None, has_post=post is not None, act=act),
        grid=(C // tc,),
        in_specs=specs,
        out_specs=[pl.BlockSpec((B, tc), lambda c: (0, c)), pl.BlockSpec((B, tc), lambda c: (0, c))],
        out_shape=[jax.ShapeDtypeStruct((B, C), out_dtype), jax.ShapeDtypeStruct((B, C), F32)],
        compiler_params=_params([_nbytes((B, tc), F32) * 8], 0, ("parallel",)),
        name="sconv_act",
    )(*args)


def _sprep_kernel(x_ref, dt_ref, alog_ref, xdt_ref, dec_ref):
    dt = dt_ref[...]
    inner = x_ref.shape[1]
    r = lax.broadcasted_iota(jnp.int32, (LANES, inner), 0)
    c = lax.broadcasted_iota(jnp.int32, (LANES, inner), 1)
    expand = (r == c // SSM_HEADDIM).astype(F32)
    dt_e = jnp.dot(dt, expand, preferred_element_type=F32, precision=HIGHEST)
    a_e = jnp.dot(dt * (-jnp.exp(alog_ref[...])), expand, preferred_element_type=F32, precision=HIGHEST)
    xdt_ref[...] = x_ref[...] * dt_e
    dec_ref[...] = jnp.exp(a_e)


def sprep(xbc_c, dt_act, a_log_row, inner):
    B = xbc_c.shape[0]
    return pl.pallas_call(
        _sprep_kernel,
        grid=(1,),
        in_specs=[pl.BlockSpec((B, inner), lambda i: (0, 0)),
                  pl.BlockSpec((B, LANES), lambda i: (0, 0)),
                  pl.BlockSpec((1, LANES), lambda i: (0, 0))],
        out_specs=[pl.BlockSpec((B, inner), lambda i: (0, 0))] * 2,
        out_shape=[jax.ShapeDtypeStruct((B, inner), F32)] * 2,
        compiler_params=_params([_nbytes((B, inner), F32) * 4], 0, ("arbitrary",)),
        name="sprep",
    )(xbc_c, dt_act, a_log_row)


def _sssd_kernel(h0_ref, xdt_ref, dec_ref, x_ref, b_ref, c_ref, z_ref, d_ref, nw_ref,
                 hn_ref, y_ref, y_s, *, heads):
    bb = h0_ref.shape[0]
    P, N = SSM_HEADDIM, SSM_STATE
    hpg = heads // SSM_GROUPS
    for h in range(heads):
        g = h // hpg
        c8 = c_ref[:, g * N:(g + 1) * N].astype(BF16)
        for b in range(bb):
            xcol = xdt_ref[h * P:(h + 1) * P, b:b + 1]
            dcol = dec_ref[h * P:(h + 1) * P, b:b + 1]
            brow = b_ref[b:b + 1, g * N:(g + 1) * N]
            hn = h0_ref[b, h] * dcol + xcol * brow
            hn_ref[b, h] = hn
            res = lax.dot_general(c8, hn.astype(BF16), (((1,), (1,)), ((), ())),
                                  preferred_element_type=F32)
            y_s[b:b + 1, h * P:(h + 1) * P] = res[b:b + 1, :]
    y = y_s[...] + x_ref[...] * d_ref[...]
    gw = hpg * P
    for g in range(SSM_GROUPS):
        sl = slice(g * gw, (g + 1) * gw)
        y_ref[:, sl] = _gated_group_norm(y[:, sl], z_ref[:, sl], nw_ref[:, sl])


def sssd_step(h0, xdt_t3, dec_t3, xbc_c, proj, z_col0, d_row, norm_w_row, heads):
    B = h0.shape[0]
    bb = SUBLANES
    inner = heads * SSM_HEADDIM
    bc = SSM_GROUPS * SSM_STATE
    st_block = (bb, heads, SSM_HEADDIM, SSM_STATE)
    blocks = [_nbytes(st_block, F32) * 2, _nbytes((inner, LANES), F32) * 2]
    return pl.pallas_call(
        functools.partial(_sssd_kernel, heads=heads),
        grid=(B // bb,),
        in_specs=[pl.BlockSpec(st_block, lambda i: (i, 0, 0, 0)),
                  pl.BlockSpec((None, inner, bb), lambda i: (i, 0, 0)),
                  pl.BlockSpec((None, inner, bb), lambda i: (i, 0, 0)),
                  pl.BlockSpec((bb, inner), lambda i: (i, 0)),
                  pl.BlockSpec((bb, bc), lambda i: (i, inner // bc)),
                  pl.BlockSpec((bb, bc), lambda i: (i, inner // bc + 1)),
                  pl.BlockSpec((bb, inner), lambda i: (i, z_col0 // inner)),
                  pl.BlockSpec((1, inner), lambda i: (0, 0)),
                  pl.BlockSpec((1, inner), lambda i: (0, 0))],
        out_specs=[pl.BlockSpec(st_block, lambda i: (i, 0, 0, 0)),
                   pl.BlockSpec((bb, inner), lambda i: (i, 0))],
        out_shape=[jax.ShapeDtypeStruct(h0.shape, F32), jax.ShapeDtypeStruct((B, inner), F32)],
        scratch_shapes=[pltpu.VMEM((bb, inner), F32)],
        compiler_params=_params(blocks, 0, ("parallel",)),
        name="sssd_step",
    )(h0, xdt_t3, dec_t3, xbc_c, xbc_c, xbc_c, proj, d_row, norm_w_row)


def _sattn_kernel(pt_ref, q_ref, kn_ref, vn_ref, lfn_ref, *rest, pages_per_step, heads, scale):
    del pt_ref
    P = pages_per_step
    ck = rest[0:P]
    cv = rest[P:2 * P]
    clf = rest[2 * P:3 * P]
    o_ref, m_s, l_s, acc_s, tail_s = rest[3 * P:]
    j = pl.program_id(1)
    R = clf[0].shape[1]

    @pl.when(j == 0)
    def _():
        s_new = jnp.sum(q_ref[...] * kn_ref[...], axis=1, keepdims=True) * scale
        m_s[...] = jnp.broadcast_to(s_new, m_s.shape)
        l_s[...] = jnp.ones_like(l_s)
        acc_s[...] = vn_ref[...]
        tail_s[...] = lfn_ref[...]

    q8 = q_ref[...].astype(BF16)
    sub = lax.broadcasted_iota(jnp.int32, (heads, R), 0)
    rr = lax.broadcasted_iota(jnp.int32, (R, R), 0)
    cc = lax.broadcasted_iota(jnp.int32, (R, R), 1)
    later = (rr > cc).astype(F32)
    for p in range(P):
        lf_t = clf[p][...]
        bias = tail_s[...] + jnp.dot(lf_t, later, preferred_element_type=F32, precision=HIGHEST)
        tail_s[...] = tail_s[...] + jnp.sum(lf_t, axis=1, keepdims=True)
        s = jnp.zeros((heads, R), F32)
        for h in range(heads):
            k_h = ck[p][pl.ds(h, R, stride=heads), :].astype(BF16)
            res = lax.dot_general(q8, k_h, (((1,), (1,)), ((), ())), preferred_element_type=F32)
            s = jnp.where(sub == h, res, s)
        s = s * scale + bias
        m_new = jnp.maximum(m_s[...], jnp.max(s, axis=1, keepdims=True))
        pe = jnp.exp(s - m_new)
        corr = jnp.exp(m_s[...] - m_new)
        l_s[...] = corr * l_s[...] + jnp.sum(pe, axis=1, keepdims=True)
        pe16 = pe.astype(BF16)
        pv = jnp.zeros((heads, FOX_HEADDIM), F32)
        for h in range(heads):
            v_h = cv[p][pl.ds(h, R, stride=heads), :].astype(BF16)
            res = jnp.dot(pe16, v_h, preferred_element_type=F32)
            pv = jnp.where(sub == h, res, pv)
        acc_s[...] = corr * acc_s[...] + pv
        m_s[...] = m_new

    @pl.when(j == pl.num_programs(1) - 1)
    def _():
        o_ref[...] = acc_s[...] / l_s[...]


def sattn(layer, q3, kn3, vn3, lfn3, cache_k, cache_v, clf_t, page_table, *, pages_per_step=4):
    B, heads, D = q3.shape
    n_pages = page_table.shape[1]
    rows = cache_k.shape[2]
    page = rows // heads
    P = pages_per_step
    while n_pages % P:
        P -= 1
    nsteps = n_pages // P

    def phys(b, j, pt, p):
        return pt[b, n_pages - 1 - (j * P + p)]

    vec_spec = pl.BlockSpec((None, heads, D), lambda b, j, pt: (b, 0, 0))
    kv_specs = [pl.BlockSpec((None, None, rows, D),
                             functools.partial(lambda b, j, pt, p: (layer, phys(b, j, pt, p), 0, 0), p=p))
                for p in range(P)]
    lf_specs = [pl.BlockSpec((None, heads, page),
                             functools.partial(lambda b, j, pt, p: (phys(b, j, pt, p), 0, 0), p=p))
                for p in range(P)]
    grid_spec = pltpu.PrefetchScalarGridSpec(
        num_scalar_prefetch=1,
        grid=(B, nsteps),
        in_specs=[vec_spec] * 4 + kv_specs + kv_specs + lf_specs,
        out_specs=vec_spec,
        scratch_shapes=[pltpu.VMEM((heads, D), F32)] * 4,
    )
    blocks = [_nbytes((rows, D), F32) * 2 * P]
    return pl.pallas_call(
        functools.partial(_sattn_kernel, pages_per_step=P, heads=heads, scale=D ** -0.5),
        grid_spec=grid_spec,
        out_shape=jax.ShapeDtypeStruct((B, heads, D), F32),
        compiler_params=_params(blocks, 0, ("parallel", "arbitrary")),
        name="sattn",
    )(page_table, q3, kn3, vn3, lfn3, *([cache_k] * P), *([cache_v] * P), *([clf_t] * P))


def _row(v, width=None):
    v = v.reshape(1, -1).astype(F32)
    if width is not None and v.shape[1] < width:
        v = jnp.pad(v, ((0, 0), (0, width - v.shape[1])))
    return v


def _layer(x2d, G, T, lw, dims, past):
    inner, xbc_w, heads, fox_w, fox_h, sc_w, d_ff, d_model = dims
    Z0, XBC0 = 0, inner
    Q0 = XBC0 + xbc_w
    K0, V0 = Q0 + fox_w, Q0 + 2 * fox_w
    SCB0 = V0 + fox_w
    SCC0, SCX0 = SCB0 + sc_w, SCB0 + 2 * sc_w
    GATE0 = SCX0 + sc_w
    M = G * T
    prompt = past is None

    proj = mm_plain(x2d, lw["w_main"], tm=1024, tn=512)
    small = mm_plain(x2d, lw["w_small"], tm=1024, tn=LANES)
    k_new = proj[:, K0:K0 + fox_w].reshape(G, T, fox_h, FOX_HEADDIM)
    v_new = proj[:, V0:V0 + fox_w].reshape(G, T, fox_h, FOX_HEADDIM)

    if prompt:
        act, cum = small_ops(G, T, small, lw["small_bias"], heads)
    else:
        act, cum = small_ops(1, M, small, lw["small_bias"], heads)
    logf = act[:, heads:heads + fox_h].reshape(G, T, fox_h)

    if prompt:
        xbc_c, xbc_tail = conv_act(G, T, xbc_w, proj, XBC0, lw["ssm_conv_w"], bias=lw["ssm_conv_b"],
                                   act="silu")
        wlen = lw["ssm_conv_w"].shape[0] - 1
        ssm_conv = xbc_tail[:, SUBLANES - wlen:, :]
        y_a, st_t = ssd_prompt(G, T, xbc_c, act, proj, Z0, lw["a_log"], lw["d_row"], lw["norm_w"], heads)
        ssm_state = jnp.swapaxes(st_t, 2, 3)
        cum_t = cum.T
        y_b = fox_prompt(G, T, proj, Q0, K0, V0, cum, cum_t, heads, fox_h)
        y_c, sc_tail = conv_act(G, T, sc_w, proj, SCC0, lw["sc_conv_w"], m=proj, m_col0=SCX0,
                                post=proj, post_col0=SCB0, out_dtype=BF16)
        sc_conv = sc_tail[:, SUBLANES - (lw["sc_conv_w"].shape[0] - 1):, :]
    else:
        prev = jnp.swapaxes(past["ssm_conv"], 0, 1)
        xbc_c, _ = sconv_act(prev, xbc_w, proj, XBC0, lw["ssm_conv_w"], bias=lw["ssm_conv_b"], act="silu")
        ssm_conv = jnp.concatenate([past["ssm_conv"][:, 1:], proj[:, None, XBC0:XBC0 + xbc_w]], axis=1)
        xdt, dec = sprep(xbc_c, act, lw["a_log"], inner)
        to_t3 = lambda v: jnp.swapaxes(v.reshape(M // SUBLANES, SUBLANES, inner), 1, 2)
        ssm_state, y_a = sssd_step(past["ssm"], to_t3(xdt), to_t3(dec), xbc_c, proj, Z0,
                                   lw["d_row"], lw["norm_w"], heads)
        as3 = lambda c0: proj[:, c0:c0 + fox_w].reshape(M, fox_h, FOX_HEADDIM)
        lfn3 = jnp.broadcast_to(act[:, heads:heads + fox_h, None], (M, fox_h, FOX_HEADDIM))
        o3 = sattn(past["layer"], as3(Q0), as3(K0), as3(V0), lfn3, past["k"], past["v"], past["logf_t"],
                   past["page_table"])
        y_b = o3.reshape(M, fox_w)
        prev_sc = jnp.swapaxes(past["sc_conv"], 0, 1)
        y_c, u_sc = sconv_act(prev_sc, sc_w, proj, SCC0, lw["sc_conv_w"], m=proj, m_col0=SCX0,
                              post=proj, post_col0=SCB0)
        sc_conv = jnp.concatenate([past["sc_conv"][:, 1:], u_sc[:, None]], axis=1)

    merged = mm_gated3([y_a, y_b, y_c], [lw["w_ssm_out"], lw["w_fox_out"], lw["w_sc_out"]],
                       proj, GATE0, lw["b_gate"])
    x1 = mm_ln(merged, lw["w_o"], x2d, lw["ln1_g"], lw["ln1_b"], dims_alpha(lw))

    up = mm_plain(x1, lw["w_up"], tm=1024, tn=512)
    if prompt:
        hmid, ffn_tail = conv_act(G, T, d_ff, up, 0, lw["ffn_conv_w"], bias=lw["ffn_conv_b"],
                                  post=up, post_col0=d_ff, act="gelu", out_dtype=BF16)
        ffn_conv = ffn_tail[:, SUBLANES - (lw["ffn_conv_w"].shape[0] - 1):, :]
    else:
        prev_f = jnp.swapaxes(past["ffn_conv"], 0, 1)
        hmid, _ = sconv_act(prev_f, d_ff, up, 0, lw["ffn_conv_w"], bias=lw["ffn_conv_b"],
                            post=up, post_col0=d_ff, act="gelu")
        ffn_conv = jnp.concatenate([past["ffn_conv"][:, 1:], up[:, None, :d_ff]], axis=1)
    x2 = mm_ln(hmid, lw["w_down"], x1, lw["ln2_g"], lw["ln2_b"], dims_alpha(lw), tk=1408)
    return x2, (k_new, v_new, logf, ssm_state, ssm_conv, sc_conv, ffn_conv)


def dims_alpha(lw):
    return lw["alpha"]


def kernel(x_prompt, x_sample, cache_k, cache_v, cache_logf, state_ssm, state_ssm_conv, state_sc_conv, state_ffn_conv, page_table, w_in, b_gate, ssm_conv_w, ssm_conv_b, ssm_dt_bias, ssm_a_log, ssm_d, ssm_norm_w, w_ssm_out, fox_b_f, w_fox_out, sc_conv_w, w_sc_out, w_o, ln1_g, ln1_b, w_up, ffn_conv_w, ffn_conv_b, w_down, ln2_g, ln2_b):
    depth, d_model = w_in.shape[0], w_in.shape[1]
    inner = w_ssm_out.shape[1]
    xbc_w = ssm_conv_w.shape[2]
    heads = ssm_dt_bias.shape[1]
    fox_h = fox_b_f.shape[1]
    fox_w = w_fox_out.shape[1]
    sc_w = sc_conv_w.shape[2]
    d_ff = ffn_conv_w.shape[2]
    dims = (inner, xbc_w, heads, fox_w, fox_h, sc_w, d_ff, d_model)
    alpha = (2 * depth) ** 0.25
    assert inner == heads * SSM_HEADDIM and xbc_w == inner + 2 * SSM_GROUPS * SSM_STATE
    assert fox_w == fox_h * FOX_HEADDIM and heads + fox_h <= LANES and heads % SUBLANES == 0

    Gp, Tp, _ = x_prompt.shape
    Gs, Ts, _ = x_sample.shape
    assert Ts == 1 and Tp % SSM_CHUNK == 0
    n_phys, page = cache_k.shape[1], cache_k.shape[2]
    ck = cache_k.reshape(depth, n_phys, page * fox_h, FOX_HEADDIM)
    cv = cache_v.reshape(depth, n_phys, page * fox_h, FOX_HEADDIM)

    c_dt0 = inner + xbc_w
    c_f0 = c_dt0 + heads + 3 * fox_w

    yp = x_prompt.reshape(Gp * Tp, d_model)
    ys = x_sample.reshape(Gs * Ts, d_model)
    st_p, st_s = [], []
    for l in range(depth):
        w = w_in[l]
        w_main = jnp.concatenate([w[:, :c_dt0], w[:, c_dt0 + heads:c_f0], w[:, c_f0 + fox_h:]],
                                 axis=1).astype(BF16)
        w_small = jnp.concatenate([w[:, c_dt0:c_dt0 + heads], w[:, c_f0:c_f0 + fox_h],
                                   jnp.zeros((d_model, LANES - heads - fox_h), F32)], axis=1).astype(BF16)
        lw = {
            "w_main": w_main, "w_small": w_small,
            "small_bias": _row(jnp.concatenate([ssm_dt_bias[l], fox_b_f[l]]), LANES),
            "b_gate": _row(b_gate[l]),
            "ssm_conv_w": ssm_conv_w[l], "ssm_conv_b": _row(ssm_conv_b[l]),
            "a_log": _row(ssm_a_log[l], LANES),
            "d_row": _row(jnp.repeat(ssm_d[l], SSM_HEADDIM)),
            "norm_w": _row(ssm_norm_w[l]),
            "w_ssm_out": w_ssm_out[l].astype(BF16), "w_fox_out": w_fox_out[l].astype(BF16),
            "w_sc_out": w_sc_out[l].astype(BF16), "w_o": w_o[l].astype(BF16),
            "sc_conv_w": sc_conv_w[l],
            "ln1_g": _row(ln1_g[l]), "ln1_b": _row(ln1_b[l]),
            "w_up": w_up[l].astype(BF16),
            "ffn_conv_w": ffn_conv_w[l], "ffn_conv_b": _row(ffn_conv_b[l]),
            "w_down": w_down[l].astype(BF16),
            "ln2_g": _row(ln2_g[l]), "ln2_b": _row(ln2_b[l]),
            "alpha": alpha,
        }
        yp, sp = _layer(yp, Gp, Tp, lw, dims, None)
        past = {"layer": l, "k": ck, "v": cv, "logf_t": jnp.swapaxes(cache_logf[l], 1, 2),
                "page_table": page_table, "ssm": state_ssm[l], "ssm_conv": state_ssm_conv[l],
                "sc_conv": state_sc_conv[l], "ffn_conv": state_ffn_conv[l]}
        ys, ss = _layer(ys, Gs, Ts, lw, dims, past)
        st_p.append(sp)
        st_s.append(ss)
    stk = lambda lst, i: jnp.stack([s[i] for s in lst])
    return (yp.reshape(Gp, Tp, d_model), ys.reshape(Gs, Ts, d_model),
            stk(st_p, 0), stk(st_p, 1), stk(st_p, 2), stk(st_p, 3), stk(st_p, 4), stk(st_p, 5), stk(st_p, 6),
            stk(st_s, 0), stk(st_s, 1), stk(st_s, 2), stk(st_s, 3), stk(st_s, 4), stk(st_s, 5), stk(st_s, 6))
```
